```python
import jax, jax.numpy as jnp
from jax import lax
import numpy as np

D_MODEL = 2048
BATCH = 8
SEQ = 2048
DEPTH = 4

GRID_W = 64
CTX_LEN = 256
EPS = 1e-6

CONV_W = 4
CONV_LEFT = 2

LRU_WIDTH = D_MODEL // 2
LRU_BLOCKS = 16
LRU_BLOCK = LRU_WIDTH // LRU_BLOCKS
LRU_C = 8.0

HEAD_DIM = 128
ATT_HEADS = D_MODEL // 256
ATT_KV_HEADS = 2
ATT_GROUP = ATT_HEADS // ATT_KV_HEADS
ATT_WIDTH = ATT_HEADS * HEAD_DIM
KV_WIDTH = ATT_KV_HEADS * HEAD_DIM
WINDOW = 128
ATT_BLOCK = 128
ROPE_THETA = 10000.0

SSD_WIDTH = D_MODEL // 2
SSD_HEAD_DIM = 64
SSD_HEADS = SSD_WIDTH // SSD_HEAD_DIM
SSD_GROUPS = 2
SSD_HPG = SSD_HEADS // SSD_GROUPS
SSD_STATE = 128
SSD_CHUNK = 128
SSD_BC = SSD_GROUPS * SSD_STATE
SSD_CONV_CH = SSD_WIDTH + 2 * SSD_BC

MIX_WIDTH = LRU_WIDTH + ATT_WIDTH + SSD_WIDTH
IN_SIZES = (LRU_WIDTH, LRU_WIDTH, ATT_WIDTH, KV_WIDTH, KV_WIDTH, ATT_WIDTH, SSD_CONV_CH, SSD_WIDTH, SSD_HEADS)
IN_WIDTH = sum(IN_SIZES)

kernel_name = "hybrid_lru_swa_ssd_dit_prefix"


def rmsnorm(x, w):
    xf = x.astype(jnp.float32)
    y = xf * lax.rsqrt(jnp.mean(xf * xf, axis=-1, keepdims=True) + EPS)
    return (y * w.astype(jnp.float32)).astype(x.dtype)


def modulation(cond, ada_w, ada_b):
    m = jax.nn.silu(cond) @ ada_w + ada_b
    return jnp.split(m, 3, axis=-1)


def split_cols(p):
    idx, acc = [], 0
    for s in IN_SIZES[:-1]:
        acc += s
        idx.append(acc)
    return jnp.split(p, idx, axis=-1)


def dwconv_centred(x, w, b):
    ch = x.shape[-1]
    y = lax.conv_general_dilated(x, w[:, None, :], window_strides=(1,),
                                 padding=[(CONV_LEFT, CONV_W - 1 - CONV_LEFT)],
                                 dimension_numbers=('NWC', 'WIO', 'NWC'),
                                 feature_group_count=ch)
    return y + b


def maybe_flip(t, rev):
    return jnp.flip(t, axis=1) if rev else t


def rope_2d(n_lat):
    rows = n_lat // GRID_W
    row = jnp.repeat(jnp.arange(rows), GRID_W).astype(jnp.float32)
    col = jnp.tile(jnp.arange(GRID_W), rows).astype(jnp.float32)
    n_freq = HEAD_DIM // 4
    inv = ROPE_THETA ** (-jnp.arange(n_freq, dtype=jnp.float32) / n_freq)
    ang = jnp.concatenate([row[:, None] * inv, col[:, None] * inv], axis=-1)
    return jnp.cos(ang), jnp.sin(ang)


def apply_rope(x, cos, sin):
    x1, x2 = jnp.split(x, 2, axis=-1)
    c = cos[None, :, None, :].astype(x.dtype)
    s = sin[None, :, None, :].astype(x.dtype)
    return jnp.concatenate([x1 * c - x2 * s, x1 * s + x2 * c], axis=-1)


def linear_scan(a, b, reverse):
    def combine(l, r):
        al, bl = l
        ar, br = r
        return al * ar, ar * bl + br
    _, h = lax.associative_scan(combine, (a, b), axis=1, reverse=reverse)
    return h


def rglru_scan(xc, w_a, b_a, w_x, b_x, lam, h0, reverse):
    bsz, L, W = xc.shape
    xf = xc.astype(jnp.float32)
    xb = xf.reshape(bsz, L, LRU_BLOCKS, LRU_BLOCK)
    r = jax.nn.sigmoid(jnp.einsum('blhi,hij->blhj', xb, w_a.astype(jnp.float32)).reshape(bsz, L, W) + b_a)
    i = jax.nn.sigmoid(jnp.einsum('blhi,hij->blhj', xb, w_x.astype(jnp.float32)).reshape(bsz, L, W) + b_x)
    log_a = -LRU_C * r * jax.nn.softplus(-lam.astype(jnp.float32))
    a = jnp.exp(log_a)
    b = jnp.sqrt(-jnp.expm1(2.0 * log_a)) * (i * xf)
    first = L - 1 if reverse else 0
    b = b.at[:, first].add(a[:, first] * h0)
    h = linear_scan(a, b, reverse)
    h_last = h[:, 0] if reverse else h[:, -1]
    return h, h_last


def rglru_mixer(xc, xl, conv_w, conv_b, ga_w, ga_b, gx_w, gx_b, lam):
    xc = dwconv_centred(xc, conv_w, conv_b)
    xl = dwconv_centred(xl, conv_w, conv_b)
    bsz = xl.shape[0]
    outs_c, outs_l = [], []
    for d, rev in enumerate((False, True)):
        h0 = jnp.zeros((bsz, LRU_WIDTH), jnp.float32)
        h_c, hc_last = rglru_scan(xc, ga_w[d], ga_b[d], gx_w[d], gx_b[d], lam[d], h0, rev)
        h_l, _ = rglru_scan(xl, ga_w[d], ga_b[d], gx_w[d], gx_b[d], lam[d], hc_last, rev)
        outs_c.append(h_c)
        outs_l.append(h_l)
    return (outs_c[0] + outs_c[1]).astype(xc.dtype), (outs_l[0] + outs_l[1]).astype(xl.dtype)


def window_attention(qc, kc, vc, ql, kl, vl, sink):
    bsz, S = ql.shape[:2]
    C = kc.shape[1]
    nb = S // ATT_BLOCK
    scale = HEAD_DIM ** -0.5
    sink_f = sink.astype(jnp.float32).reshape(ATT_KV_HEADS, ATT_GROUP)
    qb = ql.reshape(bsz, nb, ATT_BLOCK, ATT_KV_HEADS, ATT_GROUP, HEAD_DIM)
    pad = ((0, 0), (ATT_BLOCK, ATT_BLOCK), (0, 0), (0, 0))
    kp = jnp.pad(kl, pad).reshape(bsz, nb + 2, ATT_BLOCK, ATT_KV_HEADS, HEAD_DIM)
    vp = jnp.pad(vl, pad).reshape(bsz, nb + 2, ATT_BLOCK, ATT_KV_HEADS, HEAD_DIM)
    kb = jnp.concatenate([kp[:, :-2], kp[:, 1:-1], kp[:, 2:]], axis=2)
    vb = jnp.concatenate([vp[:, :-2], vp[:, 1:-1], vp[:, 2:]], axis=2)
    s_lat = jnp.einsum('bnqkgd,bnjkd->bnkgqj', qb, kb).astype(jnp.float32) * scale
    s_ctx = jnp.einsum('bnqkgd,bckd->bnkgqc', qb, kc).astype(jnp.float32) * scale
    qpos = jnp.arange(nb)[:, None] * ATT_BLOCK + jnp.arange(ATT_BLOCK)[None]
    kpos = jnp.arange(nb)[:, None] * ATT_BLOCK - ATT_BLOCK + jnp.arange(3 * ATT_BLOCK)[None]
    rel = qpos[:, :, None] - kpos[:, None, :]
    valid = (jnp.abs(rel) <= WINDOW) & (kpos[:, None, :] >= 0) & (kpos[:, None, :] < S)
    s_lat = jnp.where(valid[None, :, None, None], s_lat, -jnp.inf)
    sink_l = jnp.broadcast_to(sink_f[None, None, :, :, None, None], s_lat.shape[:-1] + (1,))
    p = jax.nn.softmax(jnp.concatenate([s_lat, s_ctx, sink_l], axis=-1), axis=-1)
    nk = 3 * ATT_BLOCK
    p_lat = p[..., :nk].astype(vl.dtype)
    p_ctx = p[..., nk:nk + C].astype(vl.dtype)
    o_l = jnp.einsum('bnkgqj,bnjkd->bnqkgd', p_lat, vb) + jnp.einsum('bnkgqc,bckd->bnqkgd', p_ctx, vc)
    o_l = o_l.reshape(bsz, S, ATT_WIDTH)
    qcg = qc.reshape(bsz, C, ATT_KV_HEADS, ATT_GROUP, HEAD_DIM)
    s_cc = jnp.einsum('bqkgd,bckd->bkgqc', qcg, kc).astype(jnp.float32) * scale
    sink_c = jnp.broadcast_to(sink_f[None, :, :, None, None], s_cc.shape[:-1] + (1,))
    pc = jax.nn.softmax(jnp.concatenate([s_cc, sink_c], axis=-1), axis=-1)[..., :C].astype(vc.dtype)
    o_c = jnp.einsum('bkgqc,bckd->bqkgd', pc, vc).reshape(bsz, C, ATT_WIDTH)
    return o_c, o_l


def ssd_chunked(x, dt, A, Bm, Cm, h0):
    bsz, L = x.shape[:2]
    Q = SSD_CHUNK
    nc = L // Q
    G, R = SSD_GROUPS, SSD_HPG
    x = x.astype(jnp.float32).reshape(bsz, nc, Q, G, R, SSD_HEAD_DIM)
    dt = dt.reshape(bsz, nc, Q, G, R)
    Bm = Bm.astype(jnp.float32).reshape(bsz, nc, Q, G, SSD_STATE)
    Cm = Cm.astype(jnp.float32).reshape(bsz, nc, Q, G, SSD_STATE)
    Acs = jnp.cumsum(dt * A.reshape(G, R), axis=2)
    A_last = Acs[:, :, -1]
    seg = Acs[:, :, :, None] - Acs[:, :, None, :]
    tril = jnp.tril(jnp.ones((Q, Q), bool))[:, :, None, None]
    decay = jnp.exp(jnp.where(tril, seg, -jnp.inf))
    CB = jnp.einsum('bcign,bcjgn->bcijg', Cm, Bm)
    Wm = CB[..., None] * decay * dt[:, :, None]
    y_diag = jnp.einsum('bcijgr,bcjgrp->bcigrp', Wm, x)
    w_state = jnp.exp(A_last[:, :, None] - Acs) * dt
    states = jnp.einsum('bcjgn,bcjgrp->bcgrpn', Bm, x * w_state[..., None])
    def step(h, inp):
        dec, st = inp
        return jnp.exp(dec)[..., None, None] * h + st, h
    h_final, h_in = lax.scan(step, h0, (jnp.moveaxis(A_last, 1, 0), jnp.moveaxis(states, 1, 0)))
    h_in = jnp.moveaxis(h_in, 0, 1)
    y_off = jnp.einsum('bcign,bcgrpn->bcigrp', Cm, h_in) * jnp.exp(Acs)[..., None]
    y = (y_diag + y_off).reshape(bsz, L, SSD_HEADS, SSD_HEAD_DIM)
    return y, h_final


def ssd_prep(xbc, conv_w, conv_b):
    xbc = jax.nn.silu(dwconv_centred(xbc, conv_w, conv_b))
    xs, Bm, Cm = jnp.split(xbc, [SSD_WIDTH, SSD_WIDTH + SSD_BC], axis=-1)
    bsz, L = xs.shape[:2]
    return (xs.reshape(bsz, L, SSD_HEADS, SSD_HEAD_DIM),
            Bm.reshape(bsz, L, SSD_GROUPS, SSD_STATE),
            Cm.reshape(bsz, L, SSD_GROUPS, SSD_STATE))


def ssd_mixer(xbc_c, z_c, dt_c, xbc_l, z_l, dt_l, conv_w, conv_b, dt_bias, A_log, D_skip, norm_w):
    xc, Bc, Cc = ssd_prep(xbc_c, conv_w, conv_b)
    xl, Bl, Cl = ssd_prep(xbc_l, conv_w, conv_b)
    bsz = xl.shape[0]
    Df = D_skip.astype(jnp.float32)[:, None]
    y_c = Df * xc.astype(jnp.float32)
    y_l = Df * xl.astype(jnp.float32)
    for d, rev in enumerate((False, True)):
        A = -jnp.exp(A_log[d].astype(jnp.float32))
        dtc = jax.nn.softplus(dt_c.astype(jnp.float32) + dt_bias[d])
        dtl = jax.nn.softplus(dt_l.astype(jnp.float32) + dt_bias[d])
        h0 = jnp.zeros((bsz, SSD_GROUPS, SSD_HPG, SSD_HEAD_DIM, SSD_STATE), jnp.float32)
        yc_d, hc = ssd_chunked(maybe_flip(xc, rev), maybe_flip(dtc, rev), A,
                               maybe_flip(Bc, rev), maybe_flip(Cc, rev), h0)
        yl_d, _ = ssd_chunked(maybe_flip(xl, rev), maybe_flip(dtl, rev), A,
                              maybe_flip(Bl, rev), maybe_flip(Cl, rev), hc)
        y_c = y_c + maybe_flip(yc_d, rev)
        y_l = y_l + maybe_flip(yl_d, rev)
    y_c = y_c.reshape(bsz, -1, SSD_WIDTH).astype(xbc_c.dtype)
    y_l = y_l.reshape(bsz, -1, SSD_WIDTH).astype(xbc_l.dtype)
    return rmsnorm(y_c * jax.nn.silu(z_c), norm_w), rmsnorm(y_l * jax.nn.silu(z_l), norm_w)


def hybrid_layer(ctx_h, x, c, c_ctx, cos, sin, update_ctx,
                 norm_w, ada_w, ada_b, w_in,
                 lru_conv_w, lru_conv_b, lru_ga_w, lru_ga_b, lru_gx_w, lru_gx_b, lru_lambda,
                 att_q_norm, att_k_norm, att_sink,
                 ssd_conv_w, ssd_conv_b, ssd_dt_bias, ssd_A_log, ssd_D, ssd_norm_w, w_out):
    bsz, S = x.shape[:2]
    C = ctx_h.shape[1]
    shift_l, scale_l, gate_l = modulation(c, ada_w, ada_b)
    shift_c, scale_c, gate_c = modulation(c_ctx, ada_w, ada_b)
    u_l = rmsnorm(x, norm_w) * (1.0 + scale_l[:, None]) + shift_l[:, None]
    u_c = rmsnorm(ctx_h, norm_w) * (1.0 + scale_c) + shift_c
    lx_c, lg_c, q_c, k_c, v_c, ag_c, xbc_c, z_c, dt_c = split_cols(u_c @ w_in)
    lx_l, lg_l, q_l, k_l, v_l, ag_l, xbc_l, z_l, dt_l = split_cols(u_l @ w_in)
    lru_c, lru_l = rglru_mixer(lx_c, lx_l, lru_conv_w, lru_conv_b, lru_ga_w, lru_ga_b,
                               lru_gx_w, lru_gx_b, lru_lambda)
    qc = rmsnorm(q_c.reshape(bsz, C, ATT_HEADS, HEAD_DIM), att_q_norm)
    kc = rmsnorm(k_c.reshape(bsz, C, ATT_KV_HEADS, HEAD_DIM), att_k_norm)
    vc = v_c.reshape(bsz, C, ATT_KV_HEADS, HEAD_DIM)
    ql = apply_rope(rmsnorm(q_l.reshape(bsz, S, ATT_HEADS, HEAD_DIM), att_q_norm), cos, sin)
    kl = apply_rope(rmsnorm(k_l.reshape(bsz, S, ATT_KV_HEADS, HEAD_DIM), att_k_norm), cos, sin)
    vl = v_l.reshape(bsz, S, ATT_KV_HEADS, HEAD_DIM)
    att_c, att_l = window_attention(qc, kc, vc, ql, kl, vl, att_sink)
    ssd_c, ssd_l = ssd_mixer(xbc_c, z_c, dt_c, xbc_l, z_l, dt_l, ssd_conv_w, ssd_conv_b,
                             ssd_dt_bias, ssd_A_log, ssd_D, ssd_norm_w)
    mix_l = jnp.concatenate([lru_l * jax.nn.silu(lg_l), att_l * jax.nn.silu(ag_l), ssd_l], axis=-1)
    x_new = x + gate_l[:, None] * (mix_l @ w_out)
    if update_ctx:
        mix_c = jnp.concatenate([lru_c * jax.nn.silu(lg_c), att_c * jax.nn.silu(ag_c), ssd_c], axis=-1)
        ctx_h = ctx_h + gate_c * (mix_c @ w_out)
    return ctx_h, x_new


def setup_inputs(seed: int = 0) -> dict:
    key = jax.random.key(seed)
    ks = jax.random.split(key, 28)
    f32 = jnp.float32
    nrm = lambda k, shape, s: jax.random.normal(k, shape, f32) * s
    a0 = jax.random.uniform(ks[14], (DEPTH, 2, LRU_WIDTH), f32, 0.9, 0.999)
    s_gate = a0 ** (1.0 / LRU_C)
    lru_lambda = jnp.log(s_gate) - jnp.log1p(-s_gate)
    dt0 = jnp.exp(jax.random.uniform(ks[19], (DEPTH, 2, SSD_HEADS), f32, np.log(1e-3), np.log(1e-1)))
    ssd_dt_bias = dt0 + jnp.log(-jnp.expm1(-dt0))
    ssd_A_log = jnp.log(jax.random.uniform(ks[20], (DEPTH, 2, SSD_HEADS), f32, 1.0, 16.0))
    return {
        'x': nrm(ks[0], (BATCH, SEQ, D_MODEL), 1.0),
        'c': nrm(ks[1], (BATCH, D_MODEL), 1.0),
        'ctx': nrm(ks[2], (BATCH, CTX_LEN, D_MODEL), 1.0),
        'c_ctx': nrm(ks[3], (D_MODEL,), 1.0),
        'norm_w': 1.0 + nrm(ks[4], (DEPTH, D_MODEL), 0.02),
        'ada_w': nrm(ks[5], (DEPTH, D_MODEL, 3 * D_MODEL), 0.5 * D_MODEL ** -0.5),
        'ada_b': nrm(ks[6], (DEPTH, 3 * D_MODEL), 0.01),
        'w_in': nrm(ks[7], (DEPTH, D_MODEL, IN_WIDTH), D_MODEL ** -0.5),
        'lru_conv_w': nrm(ks[8], (DEPTH, CONV_W, LRU_WIDTH), CONV_W ** -0.5),
        'lru_conv_b': nrm(ks[9], (DEPTH, LRU_WIDTH), 0.01),
        'lru_ga_w': nrm(ks[10], (DEPTH, 2, LRU_BLOCKS, LRU_BLOCK, LRU_BLOCK), LRU_BLOCK ** -0.5),
        'lru_ga_b': nrm(ks[11], (DEPTH, 2, LRU_WIDTH), 0.01),
        'lru_gx_w': nrm(ks[12], (DEPTH, 2, LRU_BLOCKS, LRU_BLOCK, LRU_BLOCK), LRU_BLOCK ** -0.5),
        'lru_gx_b': nrm(ks[13], (DEPTH, 2, LRU_WIDTH), 0.01),
        'lru_lambda': lru_lambda,
        'att_q_norm': 1.0 + nrm(ks[15], (DEPTH, HEAD_DIM), 0.02),
        'att_k_norm': 1.0 + nrm(ks[16], (DEPTH, HEAD_DIM), 0.02),
        'att_sink': nrm(ks[17], (DEPTH, ATT_HEADS), 0.5),
        'ssd_conv_w': nrm(ks[18], (DEPTH, CONV_W, SSD_CONV_CH), CONV_W ** -0.5),
        'ssd_conv_b': nrm(ks[21], (DEPTH, SSD_CONV_CH), 0.01),
        'ssd_dt_bias': ssd_dt_bias,
        'ssd_A_log': ssd_A_log,
        'ssd_D': 1.0 + nrm(ks[22], (DEPTH, SSD_HEADS), 0.1),
        'ssd_norm_w': 1.0 + nrm(ks[23], (DEPTH, SSD_WIDTH), 0.02),
        'w_out': nrm(ks[24], (DEPTH, MIX_WIDTH, D_MODEL), MIX_WIDTH ** -0.5),
    }


def reference(x, c, ctx, c_ctx, norm_w, ada_w, ada_b, w_in,
              lru_conv_w, lru_conv_b, lru_ga_w, lru_ga_b, lru_gx_w, lru_gx_b, lru_lambda,
              att_q_norm, att_k_norm, att_sink,
              ssd_conv_w, ssd_conv_b, ssd_dt_bias, ssd_A_log, ssd_D, ssd_norm_w, w_out):
    cos, sin = rope_2d(x.shape[1])
    ctx_h = ctx
    for l in range(DEPTH):
        ctx_h, x = hybrid_layer(
            ctx_h, x, c, c_ctx, cos, sin, l < DEPTH - 1,
            norm_w[l], ada_w[l], ada_b[l], w_in[l],
            lru_conv_w[l], lru_conv_b[l], lru_ga_w[l], lru_ga_b[l], lru_gx_w[l], lru_gx_b[l], lru_lambda[l],
            att_q_norm[l], att_k_norm[l], att_sink[l],
            ssd_conv_w[l], ssd_conv_b[l], ssd_dt_bias[l], ssd_A_log[l], ssd_D[l], ssd_norm_w[l], w_out[l])
    return x
```

```python
import functools

import jax
import jax.numpy as jnp
from jax import lax
from jax.experimental import pallas as pl
from jax.experimental.pallas import tpu as pltpu

F32 = jnp.float32
BF16 = jnp.bfloat16

LANES = 128
SUBLANES = 8
VMEM_LIMIT_BYTES = 56 * 1024 * 1024

D_MODEL = 2048
BATCH = 8
EPS = 1e-6
GRID_W = 64
CONV_W = 4
LRU_WIDTH = 1024
LRU_BLOCK = 64
LRU_C = 8.0
HEAD_DIM = 128
ATT_HEADS = 8
ATT_KV_HEADS = 2
ATT_GROUP = ATT_HEADS // ATT_KV_HEADS
WINDOW = 128
ROPE_THETA = 10000.0
SSD_WIDTH = 1024
SSD_HEAD_DIM = 64
SSD_HEADS = 16
SSD_GROUPS = 2
SSD_HPG = SSD_HEADS // SSD_GROUPS
SSD_STATE = 128
MIX_WIDTH = 3072

CHUNK = 128
ROWS = CHUNK * BATCH

S_LX, S_LG, S_Q, S_AG, S_XS, S_Z, S_K, S_V, S_BM, S_CM = 0, 8, 16, 24, 32, 40, 48, 50, 52, 54
N_SLABS = 56
NEG = -1e30


def _cparams(sem):
    return pltpu.CompilerParams(dimension_semantics=sem, vmem_limit_bytes=VMEM_LIMIT_BYTES)


def _sigmoid(x):
    return 1.0 / (1.0 + jnp.exp(-x))


def _silu(x):
    return x * _sigmoid(x)


def _softplus(x):
    return jnp.maximum(x, 0.0) + jnp.log(1.0 + jnp.exp(-jnp.abs(x)))


def _chunk_of(d, s, n_ctx_chunks, n_chunks):
    rev = jnp.where(s < n_ctx_chunks, n_ctx_chunks - 1 - s, n_chunks - 1 - (s - n_ctx_chunks))
    return jnp.where(d == 0, s, rev)


def _mod_kernel(c_ref, w_ref, b_ref, o_ref):
    s = _silu(c_ref[...]).astype(BF16)
    o_ref[...] = jnp.dot(s, w_ref[...].astype(BF16), preferred_element_type=F32) + b_ref[...]


def _modulation(cond, ada_w, ada_b):
    depth = ada_w.shape[0]
    tn = 768
    return pl.pallas_call(
        _mod_kernel,
        grid=(depth, 3 * D_MODEL // tn),
        in_specs=[
            pl.BlockSpec((2 * SUBLANES, D_MODEL), lambda l, j: (0, 0)),
            pl.BlockSpec((None, D_MODEL, tn), lambda l, j: (l, 0, j)),
            pl.BlockSpec((None, 1, tn), lambda l, j: (l, 0, j)),
        ],
        out_specs=pl.BlockSpec((None, 2 * SUBLANES, tn), lambda l, j: (l, 0, j)),
        out_shape=jax.ShapeDtypeStruct((depth, 2 * SUBLANES, 3 * D_MODEL), F32),
        compiler_params=_cparams(("arbitrary", "arbitrary")),
    )(cond, ada_w, ada_b.reshape(depth, 1, 3 * D_MODEL))


IN_TM = ROWS
IN_TN = 512
IN_SUB = 128


def _inproj_kernel(x_ref, mod_ref, nw_ref, w_ref, wdt_ref, o_ref, odt_ref, u_ref):
    j = pl.program_id(1)

    @pl.when(j == 0)
    def _():
        shift = mod_ref[:, 0:D_MODEL]
        scale1 = 1.0 + mod_ref[:, D_MODEL:2 * D_MODEL]
        nw = nw_ref[...]

        def body(k, carry):
            r0 = pl.multiple_of(k * IN_SUB, IN_SUB)
            x = x_ref[pl.ds(r0, IN_SUB), :]
            ms = jnp.mean(x * x, axis=-1, keepdims=True)
            y = x * lax.rsqrt(ms + EPS) * nw
            y = y.reshape(IN_SUB // BATCH, BATCH, D_MODEL) * scale1[None] + shift[None]
            u_ref[pl.ds(r0, IN_SUB), :] = y.reshape(IN_SUB, D_MODEL).astype(BF16)
            return carry

        lax.fori_loop(0, IN_TM // IN_SUB, body, 0)
        odt_ref[...] = jnp.dot(u_ref[...], wdt_ref[...], preferred_element_type=F32)

    acc = jnp.dot(u_ref[...], w_ref[...], preferred_element_type=F32)
    for s in range(IN_TN // LANES):
        o_ref[s] = acc[:, s * LANES:(s + 1) * LANES]


def _inproj(h, mod, norm_w, w, wdt, n_ctx_chunks):
    rows = h.shape[0]
    n_cols = w.shape[1]
    return pl.pallas_call(
        _inproj_kernel,
        grid=(rows // IN_TM, n_cols // IN_TN),
        in_specs=[
            pl.BlockSpec((IN_TM, D_MODEL), lambda i, j: (i, 0)),
            pl.BlockSpec((None, BATCH, 3 * D_MODEL), lambda i, j: (jnp.where(i < n_ctx_chunks, 0, 1), 0, 0)),
            pl.BlockSpec((1, D_MODEL), lambda i, j: (0, 0)),
            pl.BlockSpec((D_MODEL, IN_TN), lambda i, j: (0, j)),
            pl.BlockSpec((D_MODEL, SSD_GROUPS * LANES), lambda i, j: (0, 0)),
        ],
        out_specs=[
            pl.BlockSpec((IN_TN // LANES, IN_TM, LANES), lambda i, j: (j, i, 0)),
            pl.BlockSpec((IN_TM, SSD_GROUPS * LANES), lambda i, j: (i, 0)),
        ],
        out_shape=[
            jax.ShapeDtypeStruct((n_cols // LANES, rows, LANES), F32),
            jax.ShapeDtypeStruct((rows, SSD_GROUPS * LANES), F32),
        ],
        scratch_shapes=[pltpu.VMEM((IN_TM, D_MODEL), BF16)],
        compiler_params=_cparams(("arbitrary", "arbitrary")),
    )(h, mod, norm_w, w, wdt)


HALO_PREV = 2 * BATCH
HALO_NEXT = BATCH


def _conv_chunk(stage_ref, sl, x, prev, nxt, w, b):
    stage_ref[sl, 0:HALO_PREV, :] = prev
    stage_ref[sl, HALO_PREV:HALO_PREV + ROWS, :] = x
    stage_ref[sl, HALO_PREV + ROWS:HALO_PREV + ROWS + HALO_NEXT, :] = nxt
    acc = b + w[0:1] * stage_ref[sl, 0:ROWS, :]
    for k in range(1, CONV_W):
        acc = acc + w[k:k + 1] * stage_ref[sl, k * BATCH:k * BATCH + ROWS, :]
    return acc


def _halo_specs(n_slab_blk, slab_index, chunk_fn, rows):
    prev_per_chunk = ROWS // HALO_PREV
    next_per_chunk = ROWS // HALO_NEXT
    last_next = rows // HALO_NEXT - 1
    cur = pl.BlockSpec((n_slab_blk, ROWS, LANES), lambda d, g, s: (slab_index(g), chunk_fn(d, s), 0))
    prev = pl.BlockSpec(
        (n_slab_blk, HALO_PREV, LANES),
        lambda d, g, s: (slab_index(g), jnp.maximum(chunk_fn(d, s) * prev_per_chunk - 1, 0), 0))
    nxt = pl.BlockSpec(
        (n_slab_blk, HALO_NEXT, LANES),
        lambda d, g, s: (slab_index(g), jnp.minimum((chunk_fn(d, s) + 1) * next_per_chunk, last_next), 0))
    return [cur, prev, nxt]


def _segment_edges(chunk, n_ctx_chunks, n_chunks):
    first = jnp.logical_or(chunk == 0, chunk == n_ctx_chunks)
    last = jnp.logical_or(chunk == n_ctx_chunks - 1, chunk == n_chunks - 1)
    return first, last


LRU_SLABS = 4


def _lru_kernel(n_ctx_chunks, n_chunks,
                x_ref, xp_ref, xn_ref, cw_ref, cb_ref, wg_ref, gb_ref, lam_ref,
                o_ref, stage_ref, a_ref, b_ref, h_ref):
    d = pl.program_id(0)
    s = pl.program_id(2)
    chunk = _chunk_of(d, s, n_ctx_chunks, n_chunks)
    first, last = _segment_edges(chunk, n_ctx_chunks, n_chunks)

    @pl.when(s == 0)
    def _():
        h_ref[...] = jnp.zeros_like(h_ref)

    for sl in range(LRU_SLABS):
        prev = jnp.where(first, 0.0, xp_ref[sl])
        nxt = jnp.where(last, 0.0, xn_ref[sl])
        xc = _conv_chunk(stage_ref, sl, x_ref[sl], prev, nxt, cw_ref[sl], cb_ref[sl])
        gates = jnp.dot(xc.astype(BF16), wg_ref[sl], preferred_element_type=F32) + gb_ref[sl]
        r = _sigmoid(gates[:, 0:LANES])
        i = _sigmoid(gates[:, LANES:2 * LANES])
        log_a = (-LRU_C * _softplus(-lam_ref[sl])) * r
        a = jnp.exp(log_a)
        a_ref[sl] = a
        b_ref[sl] = jnp.sqrt(1.0 - a * a) * (i * xc)

    def step(t, hs):
        te = t + d * (CHUNK - 1 - 2 * t)
        row = pl.multiple_of(te * BATCH, BATCH)
        out = []
        for sl in range(LRU_SLABS):
            h = a_ref[sl, pl.ds(row, BATCH), :] * hs[sl] + b_ref[sl, pl.ds(row, BATCH), :]
            o_ref[sl, pl.ds(row, BATCH), :] = h
            out.append(h)
        return tuple(out)

    hs = lax.fori_loop(0, CHUNK, step, tuple(h_ref[sl] for sl in range(LRU_SLABS)), unroll=8)
    for sl in range(LRU_SLABS):
        h_ref[sl] = hs[sl]


def _lru(p, cw, cb, wg, gb, lam, n_ctx_chunks):
    rows = p.shape[1]
    n_chunks = rows // ROWS
    n_groups = (LRU_WIDTH // LANES) // LRU_SLABS
    chunk_fn = lambda d, s: _chunk_of(d, s, n_ctx_chunks, n_chunks)
    slab_index = lambda g: S_LX // LRU_SLABS + g
    wspec = lambda shape: pl.BlockSpec((LRU_SLABS,) + shape, lambda d, g, s: (g, 0, 0))
    dspec = lambda shape: pl.BlockSpec((None, LRU_SLABS) + shape, lambda d, g, s: (d, g, 0, 0))
    return pl.pallas_call(
        functools.partial(_lru_kernel, n_ctx_chunks, n_chunks),
        grid=(2, n_groups, n_chunks),
        in_specs=_halo_specs(LRU_SLABS, slab_index, chunk_fn, rows) + [
            wspec((CONV_W, LANES)), wspec((1, LANES)),
            dspec((LANES, 2 * LANES)), dspec((1, 2 * LANES)), dspec((1, LANES)),
        ],
        out_specs=pl.BlockSpec((None, LRU_SLABS, ROWS, LANES), lambda d, g, s: (d, g, chunk_fn(d, s), 0)),
        out_shape=jax.ShapeDtypeStruct((2, LRU_WIDTH // LANES, rows, LANES), F32),
        scratch_shapes=[
            pltpu.VMEM((LRU_SLABS, ROWS + HALO_PREV + HALO_NEXT, LANES), F32),
            pltpu.VMEM((LRU_SLABS, ROWS, LANES), F32),
            pltpu.VMEM((LRU_SLABS, ROWS, LANES), F32),
            pltpu.VMEM((LRU_SLABS, BATCH, LANES), F32),
        ],
        compiler_params=_cparams(("arbitrary", "arbitrary", "arbitrary")),
    )(p, p, p, cw, cb, wg, gb, lam)


def _rms_lanes(x, w):
    return x * lax.rsqrt(jnp.mean(x * x, axis=-1, keepdims=True) + EPS) * w


def _rope(x, cos2, sin2):
    return x * cos2 + pltpu.roll(x, HEAD_DIM // 2, 1) * sin2


def _att_kernel(n_ctx_chunks, n_chunks,
                q_ref, kp_ref, kc_ref, kn_ref, vp_ref, vc_ref, vn_ref, kx_ref, vx_ref,
                cos_ref, sin_ref, qw_ref, kw_ref, sink_ref, o_ref):
    r = pl.program_id(1)
    ctx_len = n_ctx_chunks * CHUNK
    lo = n_ctx_chunks
    hi = n_chunks - 1
    is_lat = r >= lo
    rp = jnp.clip(r - 1, lo, hi)
    rc = jnp.clip(r, lo, hi)
    rn = jnp.clip(r + 1, lo, hi)
    ok_p = jnp.logical_and(is_lat, r - 1 >= lo)
    ok_n = jnp.logical_and(is_lat, r + 1 <= hi)

    def table(ref, blk):
        return ref[pl.ds(pl.multiple_of(blk * CHUNK, CHUNK), CHUNK), :]

    cos_q, sin_q = table(cos_ref, r), table(sin_ref, r)
    cos_k = [table(cos_ref, b) for b in (rp, rc, rn)]
    sin_k = [table(sin_ref, b) for b in (rp, rc, rn)]
    qw = qw_ref[...]
    kw = kw_ref[...]
    scale = HEAD_DIM ** -0.5

    n_q = ATT_GROUP * CHUNK
    n_k = 3 * CHUNK
    qi = lax.broadcasted_iota(jnp.int32, (n_q, n_k), 0) % CHUNK
    kj = lax.broadcasted_iota(jnp.int32, (n_q, n_k), 1)
    as_i32 = lambda flag: flag.astype(jnp.int32)
    blk_ok = jnp.where(kj < CHUNK, as_i32(ok_p), jnp.where(kj < 2 * CHUNK, as_i32(is_lat), as_i32(ok_n)))
    valid = jnp.logical_and(jnp.logical_and(kj >= qi, kj <= qi + 2 * WINDOW), blk_ok > 0)
    sink = jnp.concatenate(
        [jnp.broadcast_to(sink_ref[g:g + 1, 0:1], (CHUNK, 1)) for g in range(ATT_GROUP)], axis=0)

    def gather(ref, b, n):
        return ref[pl.ds(b, n, stride=BATCH), :]

    def body(b, carry):
        q = jnp.concatenate(
            [_rope(_rms_lanes(gather(q_ref.at[g], b, CHUNK), qw), cos_q, sin_q) * scale
             for g in range(ATT_GROUP)], axis=0).astype(BF16)
        k = jnp.concatenate(
            [_rope(_rms_lanes(gather(ref.at[0], b, CHUNK), kw), cos_k[n], sin_k[n])
             for n, ref in enumerate((kp_ref, kc_ref, kn_ref))], axis=0).astype(BF16)
        v = jnp.concatenate(
            [gather(ref.at[0], b, CHUNK) for ref in (vp_ref, vc_ref, vn_ref)], axis=0).astype(BF16)
        kx = _rms_lanes(gather(kx_ref.at[0], b, ctx_len), kw).astype(BF16)
        vx = gather(vx_ref.at[0], b, ctx_len).astype(BF16)

        nt = (((1,), (1,)), ((), ()))
        s_lat = lax.dot_general(q, k, nt, preferred_element_type=F32)
        s_lat = jnp.where(valid, s_lat, NEG)
        s_ctx = lax.dot_general(q, kx, nt, preferred_element_type=F32)
        m = jnp.maximum(jnp.maximum(jnp.max(s_lat, axis=-1, keepdims=True),
                                    jnp.max(s_ctx, axis=-1, keepdims=True)), sink)
        p_lat = jnp.exp(s_lat - m)
        p_ctx = jnp.exp(s_ctx - m)
        den = (jnp.sum(p_lat, axis=-1, keepdims=True) + jnp.sum(p_ctx, axis=-1, keepdims=True)
               + jnp.exp(sink - m))
        o = (jnp.dot(p_lat.astype(BF16), v, preferred_element_type=F32)
             + jnp.dot(p_ctx.astype(BF16), vx, preferred_element_type=F32)) / den
        for g in range(ATT_GROUP):
            o_ref.at[g][pl.ds(b, CHUNK, stride=BATCH), :] = o[g * CHUNK:(g + 1) * CHUNK]
        return carry

    lax.fori_loop(0, BATCH, body, 0)


def _attention(p, cos2, sin2, qw, kw, sink, n_ctx_chunks):
    rows = p.shape[1]
    n_chunks = rows // ROWS
    lo, hi = n_ctx_chunks, n_chunks - 1
    ctx_rows = n_ctx_chunks * ROWS

    def kv_spec(slab0, off):
        return pl.BlockSpec((1, ROWS, LANES), lambda kh, r: (slab0 + kh, jnp.clip(r + off, lo, hi), 0))

    def ctx_spec(slab0):
        return pl.BlockSpec((1, ctx_rows, LANES), lambda kh, r: (slab0 + kh, 0, 0))

    full = lambda a: pl.BlockSpec(a.shape, lambda kh, r: (0,) * a.ndim)
    return pl.pallas_call(
        functools.partial(_att_kernel, n_ctx_chunks, n_chunks),
        grid=(ATT_KV_HEADS, n_chunks),
        in_specs=[
            pl.BlockSpec((ATT_GROUP, ROWS, LANES), lambda kh, r: (S_Q // ATT_GROUP + kh, r, 0)),
            kv_spec(S_K, -1), kv_spec(S_K, 0), kv_spec(S_K, 1),
            kv_spec(S_V, -1), kv_spec(S_V, 0), kv_spec(S_V, 1),
            ctx_spec(S_K), ctx_spec(S_V),
            full(cos2), full(sin2), full(qw), full(kw),
            pl.BlockSpec((None, ATT_GROUP, LANES), lambda kh, r: (kh, 0, 0)),
        ],
        out_specs=pl.BlockSpec((ATT_GROUP, ROWS, LANES), lambda kh, r: (kh, r, 0)),
        out_shape=jax.ShapeDtypeStruct((ATT_HEADS, rows, LANES), F32),
        compiler_params=_cparams(("arbitrary", "arbitrary")),
    )(p, p, p, p, p, p, p, p, p, cos2, sin2, qw, kw, sink)


SSD_XSLABS = SSD_HPG * SSD_HEAD_DIM // LANES
HI = lax.Precision.HIGHEST


def _ssd_kernel(n_ctx_chunks, n_chunks,
                x_ref, xp_ref, xn_ref, bm_ref, bp_ref, bn_ref, cm_ref, cp_ref, cn_ref, dt_ref,
                cwx_ref, cbx_ref, cwb_ref, cbb_ref, cwc_ref, cbc_ref, dtb_ref, alog_ref, dskip_ref,
                o_ref, stage_ref, act_ref, dts_ref, h_ref):
    d = pl.program_id(0)
    s = pl.program_id(2)
    chunk = _chunk_of(d, s, n_ctx_chunks, n_chunks)
    first, last = _segment_edges(chunk, n_ctx_chunks, n_chunks)

    @pl.when(s == 0)
    def _():
        h_ref[...] = jnp.zeros_like(h_ref)

    def conv_silu(k, cur, prev, nxt, w, b):
        prev = jnp.where(first, 0.0, prev)
        nxt = jnp.where(last, 0.0, nxt)
        act_ref[k] = _silu(_conv_chunk(stage_ref, k, cur, prev, nxt, w, b))

    for sl in range(SSD_XSLABS):
        conv_silu(sl, x_ref[sl], xp_ref[sl], xn_ref[sl], cwx_ref[sl], cbx_ref[sl])
    conv_silu(SSD_XSLABS, bm_ref[0], bp_ref[0], bn_ref[0], cwb_ref[0], cbb_ref[0])
    conv_silu(SSD_XSLABS + 1, cm_ref[0], cp_ref[0], cn_ref[0], cwc_ref[0], cbc_ref[0])
    dts_ref[...] = _softplus(dt_ref[...] + dtb_ref[...])
    a_neg = -jnp.exp(alog_ref[...])

    ti = lax.broadcasted_iota(jnp.int32, (CHUNK, CHUNK), 0)
    tj = lax.broadcasted_iota(jnp.int32, (CHUNK, CHUNK), 1)
    fwd = d == 0
    mask = jnp.where(fwd, tj - ti, ti - tj) <= 0
    tri = jnp.where(mask, 1.0, 0.0)
    hh = lax.broadcasted_iota(jnp.int32, (LANES, SSD_HPG * LANES), 0)
    hc = lax.broadcasted_iota(jnp.int32, (LANES, SSD_HPG * LANES), 1)
    expand = jnp.where(hh == hc // LANES, 1.0, 0.0)
    lane = lax.broadcasted_iota(jnp.int32, (1, LANES), 1)
    low = lane < SSD_HEAD_DIM
    nt = (((1,), (1,)), ((), ()))

    def gather(ref, b):
        return ref[pl.ds(b, CHUNK, stride=BATCH), :]

    def body(b, carry):
        dt = gather(dts_ref, b)
        p_cum = jnp.dot(tri, dt * a_neg, precision=HI, preferred_element_type=F32)
        p_end = jnp.where(fwd, p_cum[CHUNK - 1:CHUNK, :], p_cum[0:1, :])
        w_state = jnp.exp(p_end - p_cum) * dt
        p_bc = jnp.dot(p_cum, expand, precision=HI, preferred_element_type=F32)
        p_t = p_cum.T
        dt_t = dt.T
        ws_t = w_state.T
        bm = gather(act_ref.at[SSD_XSLABS], b)
        cm = gather(act_ref.at[SSD_XSLABS + 1], b)
        cb = lax.dot_general(cm.astype(BF16), bm.astype(BF16), nt, preferred_element_type=F32)
        bm_t = bm.T
        for pr in range(SSD_XSLABS):
            x = gather(act_ref.at[pr], b)
            hprev = h_ref[b, pr]
            lhs_y, lhs_s, rhs_y, rhs_s, e_end = [], [], [], [], []
            for half in range(2):
                r = 2 * pr + half
                pi = p_bc[:, r * LANES:(r + 1) * LANES]
                pj = p_t[r:r + 1, :]
                decay = jnp.exp(jnp.where(mask, pi - pj, NEG))
                w_m = cb * decay * dt_t[r:r + 1, :]
                c_s = cm * jnp.exp(pi)
                keep = low if half == 0 else jnp.logical_not(low)
                lhs_y += [w_m.astype(BF16), c_s.astype(BF16)]
                rhs_y += [jnp.where(keep, x, 0.0).astype(BF16), jnp.where(keep, hprev, 0.0).astype(BF16)]
                lhs_s.append((bm_t * ws_t[r:r + 1, :]).astype(BF16))
                rhs_s.append(jnp.where(keep, x, 0.0).astype(BF16))
                e_end.append(jnp.exp(jnp.where(fwd, pi[CHUNK - 1:CHUNK, :], pi[0:1, :])))
            y = jnp.dot(jnp.concatenate(lhs_y, axis=1), jnp.concatenate(rhs_y, axis=0),
                        preferred_element_type=F32)
            st = jnp.dot(jnp.concatenate(lhs_s, axis=1), jnp.concatenate(rhs_s, axis=0),
                         preferred_element_type=F32)
            h_ref[b, pr] = jnp.where(low, e_end[0], e_end[1]) * hprev + st
            y = y + jnp.where(fwd, dskip_ref[pr], 0.0) * x
            o_ref.at[pr][pl.ds(b, CHUNK, stride=BATCH), :] = y
        return carry

    lax.fori_loop(0, BATCH, body, 0)


def _ssd(p, dtp, cw, cb, dtb, alog, dskip, n_ctx_chunks):
    rows = p.shape[1]
    n_chunks = rows // ROWS
    chunk_fn = lambda d, s: _chunk_of(d, s, n_ctx_chunks, n_chunks)
    x_specs = _halo_specs(SSD_XSLABS, lambda g: S_XS // SSD_XSLABS + g, chunk_fn, rows)
    b_specs = _halo_specs(1, lambda g: S_BM + g, chunk_fn, rows)
    c_specs = _halo_specs(1, lambda g: S_CM + g, chunk_fn, rows)
    n_x = SSD_WIDTH // LANES
    wx = lambda shape: pl.BlockSpec((SSD_XSLABS,) + shape, lambda d, g, s: (g, 0, 0))
    wb = lambda shape: pl.BlockSpec((1,) + shape, lambda d, g, s: (n_x + g, 0, 0))
    wc = lambda shape: pl.BlockSpec((1,) + shape, lambda d, g, s: (n_x + SSD_GROUPS + g, 0, 0))
    per_dir = pl.BlockSpec((None, None, 1, LANES), lambda d, g, s: (d, g, 0, 0))
    return pl.pallas_call(
        functools.partial(_ssd_kernel, n_ctx_chunks, n_chunks),
        grid=(2, SSD_GROUPS, n_chunks),
        in_specs=x_specs + b_specs + c_specs + [
            pl.BlockSpec((ROWS, LANES), lambda d, g, s: (chunk_fn(d, s), g)),
            wx((CONV_W, LANES)), wx((1, LANES)),
            wb((CONV_W, LANES)), wb((1, LANES)),
            wc((CONV_W, LANES)), wc((1, LANES)),
            per_dir, per_dir,
            pl.BlockSpec((SSD_XSLABS, 1, LANES), lambda d, g, s: (g, 0, 0)),
        ],
        out_specs=pl.BlockSpec((None, SSD_XSLABS, ROWS, LANES), lambda d, g, s: (d, g, chunk_fn(d, s), 0)),
        out_shape=jax.ShapeDtypeStruct((2, n_x, rows, LANES), F32),
        scratch_shapes=[
            pltpu.VMEM((SSD_XSLABS + 2, ROWS + HALO_PREV + HALO_NEXT, LANES), F32),
            pltpu.VMEM((SSD_XSLABS + 2, ROWS, LANES), F32),
            pltpu.VMEM((ROWS, LANES), F32),
            pltpu.VMEM((BATCH, SSD_XSLABS, SSD_STATE, LANES), F32),
        ],
        compiler_params=_cparams(("arbitrary", "arbitrary", "arbitrary")),
    )(p, p, p, p, p, p, p, p, p, dtp, cw, cb, cw, cb, cw, cb, dtb, alog, dskip)


MIX_TM = 256
N_W = LRU_WIDTH // LANES


def _mix_kernel(lf_ref, lb_ref, lg_ref, at_ref, ag_ref, yf_ref, yb_ref, z_ref, nw_ref, o_ref):
    for sl in range(N_W):
        o_ref[:, sl * LANES:(sl + 1) * LANES] = (
            (lf_ref[sl] + lb_ref[sl]) * _silu(lg_ref[sl])).astype(BF16)
        o_ref[:, (N_W + sl) * LANES:(N_W + sl + 1) * LANES] = (
            at_ref[sl] * _silu(ag_ref[sl])).astype(BF16)
    ys = [(yf_ref[sl] + yb_ref[sl]) * _silu(z_ref[sl]) for sl in range(N_W)]
    ss = jnp.sum(ys[0] * ys[0], axis=-1, keepdims=True)
    for sl in range(1, N_W):
        ss = ss + jnp.sum(ys[sl] * ys[sl], axis=-1, keepdims=True)
    inv = lax.rsqrt(ss / SSD_WIDTH + EPS)
    for sl in range(N_W):
        o_ref[:, (2 * N_W + sl) * LANES:(2 * N_W + sl + 1) * LANES] = (
            ys[sl] * inv * nw_ref[sl]).astype(BF16)


def _mix(p, lru, att, ssd, ssd_nw):
    rows = p.shape[1]
    pslab = lambda s0: pl.BlockSpec((N_W, MIX_TM, LANES), lambda i: (s0 // N_W, i, 0))
    dslab = lambda d: pl.BlockSpec((None, N_W, MIX_TM, LANES), lambda i: (d, 0, i, 0))
    return pl.pallas_call(
        _mix_kernel,
        grid=(rows // MIX_TM,),
        in_specs=[dslab(0), dslab(1), pslab(S_LG), pslab(0), pslab(S_AG), dslab(0), dslab(1), pslab(S_Z),
                  pl.BlockSpec((N_W, 1, LANES), lambda i: (0, 0, 0))],
        out_specs=pl.BlockSpec((MIX_TM, MIX_WIDTH), lambda i: (i, 0)),
        out_shape=jax.ShapeDtypeStruct((rows, MIX_WIDTH), BF16),
        compiler_params=_cparams(("arbitrary",)),
    )(lru, lru, p, att, p, ssd, ssd, p, ssd_nw)


OUT_TM = ROWS
OUT_TN = 512


def _outproj_kernel(m_ref, w_ref, g_ref, h_ref, o_ref):
    acc = jnp.dot(m_ref[...], w_ref[...], preferred_element_type=F32)
    upd = acc.reshape(OUT_TM // BATCH, BATCH, OUT_TN) * g_ref[...][None]
    o_ref[...] = h_ref[...] + upd.reshape(OUT_TM, OUT_TN)


def _outproj(mix, w, gate, h, n_ctx_chunks):
    rows = h.shape[0]
    return pl.pallas_call(
        _outproj_kernel,
        grid=(rows // OUT_TM, D_MODEL // OUT_TN),
        in_specs=[
            pl.BlockSpec((OUT_TM, MIX_WIDTH), lambda i, j: (i, 0)),
            pl.BlockSpec((MIX_WIDTH, OUT_TN), lambda i, j: (0, j)),
            pl.BlockSpec((None, BATCH, OUT_TN), lambda i, j: (jnp.where(i < n_ctx_chunks, 0, 1), 0, j)),
            pl.BlockSpec((OUT_TM, OUT_TN), lambda i, j: (i, j)),
        ],
        out_specs=pl.BlockSpec((OUT_TM, OUT_TN), lambda i, j: (i, j)),
        out_shape=jax.ShapeDtypeStruct((rows, D_MODEL), F32),
        input_output_aliases={3: 0},
        compiler_params=_cparams(("arbitrary", "arbitrary")),
    )(mix, w, gate, h)


def _slabs(v):
    n = v.shape[-1] // LANES
    return jnp.moveaxis(v.reshape(v.shape[:-1] + (n, LANES)), -2, 0)


def _rope_tables(seq, ctx_len):
    rows = seq // GRID_W
    row = jnp.repeat(jnp.arange(rows), GRID_W).astype(F32)
    col = jnp.tile(jnp.arange(GRID_W), rows).astype(F32)
    n_freq = HEAD_DIM // 4
    inv = ROPE_THETA ** (-jnp.arange(n_freq, dtype=F32) / n_freq)
    ang = jnp.concatenate([row[:, None] * inv, col[:, None] * inv], axis=-1)
    cos, sin = jnp.cos(ang), jnp.sin(ang)
    cos2 = jnp.concatenate([cos, cos], axis=-1)
    sin2 = jnp.concatenate([-sin, sin], axis=-1)
    cos2 = jnp.concatenate([jnp.ones((ctx_len, HEAD_DIM), F32), cos2], axis=0)
    sin2 = jnp.concatenate([jnp.zeros((ctx_len, HEAD_DIM), F32), sin2], axis=0)
    return cos2, sin2


def _block_diag_pairs(w):
    w = w.reshape(LRU_WIDTH // LANES, 2, LRU_BLOCK, LRU_BLOCK)
    z = jnp.zeros_like(w[:, 0])
    top = jnp.concatenate([w[:, 0], z], axis=-1)
    bot = jnp.concatenate([z, w[:, 1]], axis=-1)
    return jnp.concatenate([top, bot], axis=-2)


def _in_weight(w_in):
    lw, aw, kv = LRU_WIDTH, ATT_HEADS * HEAD_DIM, ATT_KV_HEADS * HEAD_DIM
    bc = SSD_GROUPS * SSD_STATE
    o = 0
    parts = {}
    for name, size in (("lx", lw), ("lg", lw), ("q", aw), ("k", kv), ("v", kv), ("ag", aw),
                       ("xs", SSD_WIDTH), ("bm", bc), ("cm", bc), ("z", SSD_WIDTH), ("dt", SSD_HEADS)):
        parts[name] = w_in[:, o:o + size]
        o += size
    main = jnp.concatenate([parts[n] for n in ("lx", "lg", "q", "ag", "xs", "z", "k", "v", "bm", "cm")], axis=1)
    wdt = jnp.concatenate(
        [_pad_lanes(parts["dt"][:, g * SSD_HPG:(g + 1) * SSD_HPG]) for g in range(SSD_GROUPS)], axis=1)
    return main.astype(BF16), wdt.astype(BF16)


def _per_group_lanes(v):
    return _pad_lanes(v.reshape(2, SSD_GROUPS, SSD_HPG))[:, :, None, :]


def _pad_lanes(v):
    return jnp.pad(v, [(0, 0)] * (v.ndim - 1) + [(0, LANES - v.shape[-1])])


def kernel(x, c, ctx, c_ctx, norm_w, ada_w, ada_b, w_in, lru_conv_w, lru_conv_b, lru_ga_w, lru_ga_b, lru_gx_w, lru_gx_b, lru_lambda, att_q_norm, att_k_norm, att_sink, ssd_conv_w, ssd_conv_b, ssd_dt_bias, ssd_A_log, ssd_D, ssd_norm_w, w_out):
    bsz, seq, _ = x.shape
    ctx_len = ctx.shape[1]
    depth = norm_w.shape[0]
    assert bsz == BATCH and seq % CHUNK == 0 and ctx_len % CHUNK == 0
    n_ctx_chunks = ctx_len // CHUNK

    h = jnp.concatenate([jnp.swapaxes(ctx, 0, 1), jnp.swapaxes(x, 0, 1)], axis=0)
    h = h.reshape((ctx_len + seq) * BATCH, D_MODEL)

    cond = jnp.zeros((2 * SUBLANES, D_MODEL), F32).at[0:BATCH].set(c).at[BATCH].set(c_ctx)
    mods = _modulation(cond, ada_w, ada_b)
    mods = jnp.stack([jnp.broadcast_to(mods[:, BATCH:BATCH + 1], (depth, BATCH, 3 * D_MODEL)),
                      mods[:, 0:BATCH]], axis=1)
    cos2, sin2 = _rope_tables(seq, ctx_len)

    for l in range(depth):
        w_main, w_dt = _in_weight(w_in[l])
        p, dtp = _inproj(h, mods[l], norm_w[l].reshape(1, D_MODEL), w_main, w_dt, n_ctx_chunks)

        wg = jnp.concatenate([jax.vmap(_block_diag_pairs)(lru_ga_w[l]),
                              jax.vmap(_block_diag_pairs)(lru_gx_w[l])], axis=-1).astype(BF16)
        gb = jnp.concatenate([jnp.moveaxis(_slabs(lru_ga_b[l][:, None, :]), 0, 1),
                              jnp.moveaxis(_slabs(lru_gx_b[l][:, None, :]), 0, 1)], axis=-1)
        lam = jnp.moveaxis(_slabs(lru_lambda[l][:, None, :]), 0, 1)
        lru = _lru(p, _slabs(lru_conv_w[l]), _slabs(lru_conv_b[l][None, :]), wg, gb, lam, n_ctx_chunks)

        sink = jnp.broadcast_to(att_sink[l].reshape(ATT_KV_HEADS, ATT_GROUP, 1),
                                (ATT_KV_HEADS, ATT_GROUP, LANES))
        att = _attention(p, cos2, sin2, att_q_norm[l].reshape(1, HEAD_DIM),
                         att_k_norm[l].reshape(1, HEAD_DIM), sink, n_ctx_chunks)

        dskip = _slabs(jnp.repeat(ssd_D[l], SSD_HEAD_DIM)[None, :])
        ssd = _ssd(p, dtp, _slabs(ssd_conv_w[l]), _slabs(ssd_conv_b[l][None, :]),
                   _per_group_lanes(ssd_dt_bias[l]), _per_group_lanes(ssd_A_log[l]),
                   dskip, n_ctx_chunks)

        mix = _mix(p, lru, att, ssd, _slabs(ssd_norm_w[l][None, :]))
        h = _outproj(mix, w_out[l].astype(BF16), mods[l][:, :, 2 * D_MODEL:], h, n_ctx_chunks)

    out = h[ctx_len * BATCH:].reshape(seq, BATCH, D_MODEL)
    return jnp.swapaxes(out, 0, 1)
```
